```python
import math
import jax, jax.numpy as jnp
from jax import lax
import numpy as np

D_MODEL = 1024
BATCH = 4
SEQ = 4096
DEPTH = 4
DEC_BATCH = 32
DEC_SEQ = 1
PAST_LEN = 8192
PAGE_SIZE = 128

N_MIXERS = 2
N_MLSTM_LAYERS = (DEPTH + 1) // 2
N_DSA_LAYERS = DEPTH // 2
M_HEADS = 4
M_HEAD_DIM = 128
M_WIDTH = M_HEADS * M_HEAD_DIM
CONV_WIDTH = 4
M_CHUNK = 64
M_TOK_PROJ = 4 * M_WIDTH + 2 * M_HEADS
A_HEADS = 8
A_HEAD_DIM = 64
A_WIDTH = A_HEADS * A_HEAD_DIM
IDX_HEADS = 8
IDX_DIM = 64
IDX_TOPK = 256
Q_BLOCK = 128
A_SPLIT = (A_WIDTH, A_WIDTH, A_WIDTH, IDX_HEADS * IDX_DIM, IDX_DIM, IDX_HEADS)
A_TOK_PROJ = sum(A_SPLIT)
REL_BUCKETS = 32
REL_MAX_DIST = 128
MEM_LEN = 256
X_HEADS = 4
X_HEAD_DIM = 128
X_WIDTH = X_HEADS * X_HEAD_DIM
TOK_WIDTH = 512
M_PROJ = M_TOK_PROJ + X_WIDTH
A_PROJ = A_TOK_PROJ + X_WIDTH
N_GROUPS = 4
EXP_PER_GROUP = 8
N_EXPERTS = N_GROUPS * EXP_PER_GROUP
D_EXPERT = 256
TOP_K_EXP = 2
DN_ALPHA = (2 * DEPTH) ** 0.25
DN_BETA = (8 * DEPTH) ** -0.25
LN_EPS = 1e-5

kernel_name = "hybrid_mlstm_dsa_memxattn_hmoe_step"


def layer_norm(x, g, b):
    xf = x.astype(jnp.float32)
    mu = jnp.mean(xf, -1, keepdims=True)
    var = jnp.mean(jnp.square(xf - mu), -1, keepdims=True)
    return ((xf - mu) * lax.rsqrt(var + LN_EPS)).astype(x.dtype) * g + b


def causal_conv(u, buf, w):
    T = u.shape[1]
    ext = jnp.concatenate([buf, u], axis=1)
    out = sum(ext[:, j:j + T] * w[j] for j in range(CONV_WIDTH))
    return out, ext[:, T:]


def mlstm_chunk_step(carry, inp):
    C, n, m = carry
    q, k, v, ig, lf = inp
    L = q.shape[2]
    b = jnp.cumsum(lf, axis=-1)
    causal = jnp.tril(jnp.ones((L, L), dtype=bool))
    logd = jnp.where(causal, b[..., :, None] - b[..., None, :] + ig[..., None, :], -jnp.inf)
    log_prev = b + m[..., None]
    m_t = jnp.maximum(log_prev, jnp.max(logd, axis=-1))
    d = jnp.exp(logd - m_t[..., None])
    w_prev = jnp.exp(log_prev - m_t)
    s = jnp.einsum('bhtd,bhsd->bhts', q, k) * d
    num = w_prev[..., None] * jnp.einsum('bhtk,bhvk->bhtv', q, C) + jnp.einsum('bhts,bhsv->bhtv', s, v)
    den = w_prev * jnp.einsum('bhtk,bhk->bht', q, n) + jnp.sum(s, -1)
    h = num / jnp.maximum(jnp.abs(den), jnp.exp(-m_t))[..., None]
    m_new = m_t[..., -1]
    w_c = jnp.exp(b[..., -1] + m - m_new)
    w_s = jnp.exp(b[..., -1:] - b + ig - m_new[..., None])
    C_new = w_c[..., None, None] * C + jnp.einsum('bhsv,bhsk->bhvk', v * w_s[..., None], k)
    n_new = w_c[..., None] * n + jnp.einsum('bhs,bhsk->bhk', w_s, k)
    return (C_new, n_new, m_new), h


def mlstm_mixer(u, C, n, m, conv_buf, b_gates, conv_w, norm_g, chunk):
    B, T, _ = u.shape
    nc = T // chunk
    qk, conv_new = causal_conv(u[..., :2 * M_WIDTH], conv_buf.astype(u.dtype), conv_w)
    qk = jax.nn.silu(qk).astype(jnp.float32)
    v = u[..., 2 * M_WIDTH:3 * M_WIDTH].astype(jnp.float32)
    o = u[..., 3 * M_WIDTH:4 * M_WIDTH].astype(jnp.float32)
    gates = u[..., 4 * M_WIDTH:].astype(jnp.float32) + b_gates.astype(jnp.float32)

    def to_chunks(z):
        return z.reshape(B, nc, chunk, M_HEADS, M_HEAD_DIM).transpose(1, 0, 3, 2, 4)

    def gate_chunks(g):
        return g.reshape(B, nc, chunk, M_HEADS).transpose(1, 0, 3, 2)

    q = to_chunks(qk[..., :M_WIDTH])
    k = to_chunks(qk[..., M_WIDTH:]) * (M_HEAD_DIM ** -0.5)
    ig = gate_chunks(gates[..., :M_HEADS])
    lf = gate_chunks(jax.nn.log_sigmoid(gates[..., M_HEADS:]))
    carry0 = (C.astype(jnp.float32), n.astype(jnp.float32), m.astype(jnp.float32))
    (C1, n1, m1), h = lax.scan(mlstm_chunk_step, carry0, (q, k, to_chunks(v), ig, lf))
    h = h.transpose(1, 0, 3, 2, 4).reshape(B, T, M_HEADS, M_HEAD_DIM)
    mu = jnp.mean(h, -1, keepdims=True)
    var = jnp.mean(jnp.square(h - mu), -1, keepdims=True)
    h = ((h - mu) * lax.rsqrt(var + LN_EPS)).reshape(B, T, M_WIDTH) * norm_g
    out = (h * jax.nn.sigmoid(o)).astype(u.dtype)
    return out, (C1, n1, m1, conv_new)


def t5_bucket(dist):
    exact = REL_BUCKETS // 2
    d = jnp.maximum(dist, 0)
    far = exact + (jnp.log(jnp.maximum(d, 1).astype(jnp.float32) / exact)
                   / math.log(REL_MAX_DIST / exact) * (REL_BUCKETS - exact)).astype(jnp.int32)
    return jnp.where(d < exact, d, jnp.minimum(far, REL_BUCKETS - 1))


def dsa_split(u):
    B, T, _ = u.shape
    o = [0]
    for w in A_SPLIT:
        o.append(o[-1] + w)
    q, k, v, qi, ki, wi = [u[..., o[i]:o[i + 1]] for i in range(6)]
    return (q.reshape(B, T, A_HEADS, A_HEAD_DIM), k.reshape(B, T, A_HEADS, A_HEAD_DIM),
            v.reshape(B, T, A_HEADS, A_HEAD_DIM), qi.reshape(B, T, IDX_HEADS, IDX_DIM), ki, wi)


def index_scores(qi, wi, ki):
    s = jax.nn.relu(jnp.einsum('bthd,bsd->bths', qi, ki) * (IDX_DIM ** -0.5))
    return jnp.einsum('bth,bths->bts', wi * (IDX_HEADS ** -0.5), s).astype(jnp.float32)


def sparse_attend(q, kg, vg, sel, tpos, rel_bias):
    dist = tpos[None, :, None] - sel
    bias = jnp.swapaxes(rel_bias[t5_bucket(dist)], -1, -2).astype(jnp.float32)
    logits = jnp.einsum('bthd,btkhd->bthk', q, kg).astype(jnp.float32) * (A_HEAD_DIM ** -0.5) + bias
    logits = jnp.where((dist >= 0)[:, :, None, :], logits, -jnp.inf)
    p = jax.nn.softmax(logits, axis=-1).astype(vg.dtype)
    return jnp.einsum('bthk,btkhd->bthd', p, vg)


def dsa_prompt(q, k, v, qi, ki, wi, rel_bias):
    B, T = q.shape[:2]
    topk = min(IDX_TOPK, T // 4)
    spos = jnp.arange(T)

    def block(i):
        t0 = i * Q_BLOCK
        tpos = t0 + jnp.arange(Q_BLOCK)
        qb = lax.dynamic_slice_in_dim(q, t0, Q_BLOCK, 1)
        qib = lax.dynamic_slice_in_dim(qi, t0, Q_BLOCK, 1)
        wib = lax.dynamic_slice_in_dim(wi, t0, Q_BLOCK, 1)
        sc = index_scores(qib, wib, ki)
        sc = jnp.where(spos[None, None, :] <= tpos[None, :, None], sc, -jnp.inf)
        _, sel = lax.top_k(sc, topk)
        kg = jax.vmap(lambda a, s: a[s])(k, sel)
        vg = jax.vmap(lambda a, s: a[s])(v, sel)
        return sparse_attend(qb, kg, vg, sel, tpos, rel_bias)

    out = lax.map(block, jnp.arange(T // Q_BLOCK))
    return out.transpose(1, 0, 2, 3, 4).reshape(B, T, A_WIDTH)


def dsa_sample(q, k, v, qi, ki, wi, k_pool, v_pool, ki_pool, page_table, rel_bias):
    DB, T = q.shape[:2]
    past = page_table.shape[1] * PAGE_SIZE
    L = past + T
    topk = min(IDX_TOPK, L // 4)
    tpos = past + jnp.arange(T)
    ki_past = ki_pool[page_table].reshape(DB, past, IDX_DIM).astype(ki.dtype)
    ki_all = jnp.concatenate([ki_past, ki], axis=1)
    sc = index_scores(qi, wi, ki_all)
    sc = jnp.where(jnp.arange(L)[None, None, :] <= tpos[None, :, None], sc, -jnp.inf)
    _, sel = lax.top_k(sc, topk)
    in_past = sel < past
    sp = jnp.minimum(sel, past - 1)
    phys = jax.vmap(lambda pt, s: pt[s // PAGE_SIZE])(page_table, sp)
    row = sp % PAGE_SIZE
    sn = jnp.clip(sel - past, 0, T - 1)

    def pick(pool, new):
        from_pool = pool[phys, row].astype(new.dtype)
        from_new = jax.vmap(lambda a, s: a[s])(new, sn)
        return jnp.where(in_past[..., None, None], from_pool, from_new)

    out = sparse_attend(q, pick(k_pool, k), pick(v_pool, v), sel, tpos, rel_bias)
    return out.reshape(DB, T, A_WIDTH)


def memory_attend(qm, mk, mv):
    B, T, _ = qm.shape
    qh = qm.reshape(B, T, X_HEADS, X_HEAD_DIM)
    logits = jnp.einsum('bthd,bshd->bhts', qh, mk.astype(qm.dtype)).astype(jnp.float32) * (X_HEAD_DIM ** -0.5)
    p = jax.nn.softmax(logits, axis=-1).astype(qm.dtype)
    return jnp.einsum('bhts,bshd->bthd', p, mv.astype(qm.dtype)).reshape(B, T, X_WIDTH)


def hier_moe(x, wg, bg, we, be, w_up, w_gate, w_down):
    shp = x.shape
    xf = x.reshape(-1, D_MODEL)
    g_prob = jax.nn.softmax((xf @ wg + bg).astype(jnp.float32), axis=-1)
    g_sel = jnp.argmax(g_prob, axis=-1)
    g_w = jnp.take_along_axis(g_prob, g_sel[:, None], axis=1)
    e_logits = (xf @ we + be).astype(jnp.float32).reshape(-1, N_GROUPS, EXP_PER_GROUP)
    e_logits = jnp.take_along_axis(e_logits, g_sel[:, None, None], axis=1)[:, 0]
    top_p, top_i = lax.top_k(jax.nn.softmax(e_logits, axis=-1), TOP_K_EXP)
    top_p = top_p / jnp.sum(top_p, -1, keepdims=True)
    within = jnp.sum(jax.nn.one_hot(top_i, EXP_PER_GROUP, dtype=jnp.float32) * top_p[..., None], axis=1)
    gate = jax.nn.one_hot(g_sel, N_GROUPS, dtype=jnp.float32)[:, :, None] * (g_w * within)[:, None, :]
    out = jnp.zeros_like(xf)
    for g in range(N_GROUPS):
        sl = slice(g * EXP_PER_GROUP, (g + 1) * EXP_PER_GROUP)
        h = jax.nn.silu(jnp.einsum('nd,edf->nef', xf, w_gate[sl])) * jnp.einsum('nd,edf->nef', xf, w_up[sl])
        h = h * gate[:, g, :, None].astype(h.dtype)
        out = out + jnp.einsum('nef,efd->nd', h, w_down[sl])
    return out.reshape(shp)


def residual_layer(x, tok, mem_out, w_o, g1, b1, g2, b2, wg, bg, we, be, wu, wgt, wd):
    y = jnp.concatenate([tok, mem_out], axis=-1) @ w_o
    x = layer_norm(DN_ALPHA * x + y, g1, b1)
    return layer_norm(DN_ALPHA * x + hier_moe(x, wg, bg, we, be, wu, wgt, wd), g2, b2)


def setup_inputs(seed: int = 0) -> dict:
    key = jax.random.key(seed)
    ks = iter(jax.random.split(key, 48))

    def nrm(shape, scale):
        return jax.random.normal(next(ks), shape, jnp.float32) * scale

    n_m, n_a = N_MLSTM_LAYERS, N_DSA_LAYERS
    n_pages = PAST_LEN // PAGE_SIZE
    n_used = DEC_BATCH * n_pages
    n_pool = n_used + (n_used + 3) // 4
    page_table = jax.random.permutation(next(ks), n_pool)[:n_used].reshape(DEC_BATCH, n_pages).astype(jnp.int32)

    w_in_mlstm = nrm((n_m, D_MODEL, M_PROJ), D_MODEL ** -0.5)
    w_in_mlstm = w_in_mlstm.at[:, :, 2 * M_WIDTH:3 * M_WIDTH].multiply(DN_BETA)
    w_in_mlstm = w_in_mlstm.at[:, :, 4 * M_WIDTH:M_TOK_PROJ].multiply(0.1)
    b_gates = jnp.concatenate([nrm((n_m, M_HEADS), 0.1),
                               jnp.linspace(3.0, 6.0, M_HEADS)[None, :] + nrm((n_m, M_HEADS), 0.1)], axis=-1)
    w_in_dsa = nrm((n_a, D_MODEL, A_PROJ), D_MODEL ** -0.5)
    w_in_dsa = w_in_dsa.at[:, :, 2 * A_WIDTH:3 * A_WIDTH].multiply(DN_BETA)
    w_mem_kv = nrm((DEPTH, D_MODEL, 2 * X_WIDTH), D_MODEL ** -0.5)
    w_mem_kv = w_mem_kv.at[:, :, X_WIDTH:].multiply(DN_BETA)

    return {
        "x_prompt": nrm((BATCH, SEQ, D_MODEL), 1.0),
        "x_sample": nrm((DEC_BATCH, DEC_SEQ, D_MODEL), 1.0),
        "cache_k": nrm((n_a, n_pool, PAGE_SIZE, A_HEADS, A_HEAD_DIM), 1.0),
        "cache_v": nrm((n_a, n_pool, PAGE_SIZE, A_HEADS, A_HEAD_DIM), DN_BETA),
        "cache_kidx": nrm((n_a, n_pool, PAGE_SIZE, IDX_DIM), 1.0),
        "state_C": nrm((n_m, DEC_BATCH, M_HEADS, M_HEAD_DIM, M_HEAD_DIM), 0.3),
        "state_n": nrm((n_m, DEC_BATCH, M_HEADS, M_HEAD_DIM), 0.5),
        "state_m": nrm((n_m, DEC_BATCH, M_HEADS), 1.0),
        "state_conv": nrm((n_m, DEC_BATCH, CONV_WIDTH - 1, 2 * M_WIDTH), 1.0),
        "cache_mem_k": nrm((DEPTH, DEC_BATCH, MEM_LEN, X_HEADS, X_HEAD_DIM), 1.0),
        "cache_mem_v": nrm((DEPTH, DEC_BATCH, MEM_LEN, X_HEADS, X_HEAD_DIM), DN_BETA),
        "page_table": page_table,
        "mem_prompt": nrm((BATCH, MEM_LEN, D_MODEL), 1.0),
        "w_in_mlstm": w_in_mlstm,
        "b_gates_mlstm": b_gates,
        "conv_mlstm": nrm((n_m, CONV_WIDTH, 2 * M_WIDTH), CONV_WIDTH ** -0.5),
        "norm_mlstm": 1.0 + nrm((n_m, M_WIDTH), 0.02),
        "w_in_dsa": w_in_dsa,
        "rel_bias": nrm((REL_BUCKETS, A_HEADS), 0.2),
        "w_mem_kv": w_mem_kv,
        "w_out": nrm((DEPTH, TOK_WIDTH + X_WIDTH, D_MODEL), (TOK_WIDTH + X_WIDTH) ** -0.5 * DN_BETA),
        "ln_mix_g": 1.0 + nrm((DEPTH, D_MODEL), 0.02),
        "ln_mix_b": nrm((DEPTH, D_MODEL), 0.02),
        "ln_ffn_g": 1.0 + nrm((DEPTH, D_MODEL), 0.02),
        "ln_ffn_b": nrm((DEPTH, D_MODEL), 0.02),
        "w_router_group": nrm((DEPTH, D_MODEL, N_GROUPS), D_MODEL ** -0.5),
        "b_router_group": nrm((DEPTH, N_GROUPS), 0.01),
        "w_router_expert": nrm((DEPTH, D_MODEL, N_EXPERTS), D_MODEL ** -0.5),
        "b_router_expert": nrm((DEPTH, N_EXPERTS), 0.01),
        "w_exp_up": nrm((DEPTH, N_EXPERTS, D_MODEL, D_EXPERT), D_MODEL ** -0.5),
        "w_exp_gate": nrm((DEPTH, N_EXPERTS, D_MODEL, D_EXPERT), D_MODEL ** -0.5),
        "w_exp_down": nrm((DEPTH, N_EXPERTS, D_EXPERT, D_MODEL), D_EXPERT ** -0.5 * DN_BETA),
    }


def reference(x_prompt, x_sample, cache_k, cache_v, cache_kidx, state_C, state_n, state_m, state_conv,
              cache_mem_k, cache_mem_v, page_table, mem_prompt, w_in_mlstm, b_gates_mlstm, conv_mlstm,
              norm_mlstm, w_in_dsa, rel_bias, w_mem_kv, w_out, ln_mix_g, ln_mix_b, ln_ffn_g, ln_ffn_b,
              w_router_group, b_router_group, w_router_expert, b_router_expert, w_exp_up, w_exp_gate,
              w_exp_down):
    xp, xs = x_prompt, x_sample
    bp, ts = xp.shape[0], xs.shape[1]
    kp, vp, kip, ksm, vsm, kism = [], [], [], [], [], []
    Cp, nP, mP, cP, Cs, nS, mS, cS = [], [], [], [], [], [], [], []
    mkp, mvp = [], []
    for l in range(DEPTH):
        j = l // N_MIXERS
        kv = mem_prompt @ w_mem_kv[l]
        mk_p = kv[..., :X_WIDTH].reshape(bp, -1, X_HEADS, X_HEAD_DIM)
        mv_p = kv[..., X_WIDTH:].reshape(bp, -1, X_HEADS, X_HEAD_DIM)
        mkp.append(mk_p)
        mvp.append(mv_p)
        if l % N_MIXERS == 0:
            up_p = xp @ w_in_mlstm[j]
            up_s = xs @ w_in_mlstm[j]
            tok_p, st_p = mlstm_mixer(
                up_p[..., :M_TOK_PROJ],
                jnp.zeros((bp, M_HEADS, M_HEAD_DIM, M_HEAD_DIM), jnp.float32),
                jnp.zeros((bp, M_HEADS, M_HEAD_DIM), jnp.float32),
                jnp.zeros((bp, M_HEADS), jnp.float32),
                jnp.zeros((bp, CONV_WIDTH - 1, 2 * M_WIDTH), xp.dtype),
                b_gates_mlstm[j], conv_mlstm[j], norm_mlstm[j], M_CHUNK)
            tok_s, st_s = mlstm_mixer(up_s[..., :M_TOK_PROJ], state_C[j], state_n[j], state_m[j], state_conv[j],
                                      b_gates_mlstm[j], conv_mlstm[j], norm_mlstm[j], ts)
            Cp.append(st_p[0]); nP.append(st_p[1]); mP.append(st_p[2]); cP.append(st_p[3])
            Cs.append(st_s[0]); nS.append(st_s[1]); mS.append(st_s[2]); cS.append(st_s[3])
        else:
            up_p = xp @ w_in_dsa[j]
            up_s = xs @ w_in_dsa[j]
            q, k, v, qi, ki, wi = dsa_split(up_p[..., :A_TOK_PROJ])
            tok_p = dsa_prompt(q, k, v, qi, ki, wi, rel_bias)
            kp.append(k); vp.append(v); kip.append(ki)
            q, k, v, qi, ki, wi = dsa_split(up_s[..., :A_TOK_PROJ])
            tok_s = dsa_sample(q, k, v, qi, ki, wi, cache_k[j], cache_v[j], cache_kidx[j], page_table, rel_bias)
            ksm.append(k); vsm.append(v); kism.append(ki)
        mo_p = memory_attend(up_p[..., -X_WIDTH:], mk_p, mv_p)
        mo_s = memory_attend(up_s[..., -X_WIDTH:], cache_mem_k[l], cache_mem_v[l])
        xp = residual_layer(xp, tok_p, mo_p, w_out[l], ln_mix_g[l], ln_mix_b[l], ln_ffn_g[l], ln_ffn_b[l],
                            w_router_group[l], b_router_group[l], w_router_expert[l], b_router_expert[l],
                            w_exp_up[l], w_exp_gate[l], w_exp_down[l])
        xs = residual_layer(xs, tok_s, mo_s, w_out[l], ln_mix_g[l], ln_mix_b[l], ln_ffn_g[l], ln_ffn_b[l],
                            w_router_group[l], b_router_group[l], w_router_expert[l], b_router_expert[l],
                            w_exp_up[l], w_exp_gate[l], w_exp_down[l])
    return (xp, xs,
            jnp.stack(kp), jnp.stack(vp), jnp.stack(kip),
            jnp.stack(ksm), jnp.stack(vsm), jnp.stack(kism),
            jnp.stack(Cp), jnp.stack(nP), jnp.stack(mP), jnp.stack(cP),
            jnp.stack(Cs), jnp.stack(nS), jnp.stack(mS), jnp.stack(cS),
            jnp.stack(mkp), jnp.stack(mvp))
```

```python
import functools
import math

import jax
import jax.numpy as jnp
from jax import lax
from jax.experimental import pallas as pl
from jax.experimental.pallas import tpu as pltpu

D_MODEL = 1024
DEPTH = 4
PAGE_SIZE = 128
N_MIXERS = 2
M_HEADS = 4
M_HEAD_DIM = 128
M_WIDTH = M_HEADS * M_HEAD_DIM
CONV_WIDTH = 4
M_CHUNK = 64
M_TOK_PROJ = 4 * M_WIDTH + 2 * M_HEADS
A_HEADS = 8
A_HEAD_DIM = 64
A_WIDTH = A_HEADS * A_HEAD_DIM
IDX_HEADS = 8
IDX_DIM = 64
IDX_TOPK = 256
Q_BLOCK = 128
A_SPLIT = (A_WIDTH, A_WIDTH, A_WIDTH, IDX_HEADS * IDX_DIM, IDX_DIM, IDX_HEADS)
A_TOK_PROJ = sum(A_SPLIT)
REL_BUCKETS = 32
REL_MAX_DIST = 128
X_HEADS = 4
X_HEAD_DIM = 128
X_WIDTH = X_HEADS * X_HEAD_DIM
N_GROUPS = 4
EXP_PER_GROUP = 8
N_EXPERTS = N_GROUPS * EXP_PER_GROUP
D_EXPERT = 256
DN_ALPHA = (2 * DEPTH) ** 0.25
LN_EPS = 1e-5

MOE_TILE_ROWS = 256


def _layer_norm(x, g, b):
    mu = jnp.mean(x, -1, keepdims=True)
    var = jnp.mean(jnp.square(x - mu), -1, keepdims=True)
    return (x - mu) * lax.rsqrt(var + LN_EPS) * g + b


def _causal_conv(u, buf, w):
    T = u.shape[1]
    ext = jnp.concatenate([buf, u], axis=1)
    out = sum(ext[:, j:j + T] * w[j] for j in range(CONV_WIDTH))
    return out, ext[:, T:]


def _mlstm_chunk_step(carry, inp):
    C, n, m = carry
    q, k, v, ig, lf = inp
    L = q.shape[2]
    b = jnp.cumsum(lf, axis=-1)
    causal = jnp.tril(jnp.ones((L, L), dtype=bool))
    logd = jnp.where(causal, b[..., :, None] - b[..., None, :] + ig[..., None, :], -jnp.inf)
    log_prev = b + m[..., None]
    m_t = jnp.maximum(log_prev, jnp.max(logd, axis=-1))
    d = jnp.exp(logd - m_t[..., None])
    w_prev = jnp.exp(log_prev - m_t)
    s = jnp.einsum('bhtd,bhsd->bhts', q, k) * d
    num = w_prev[..., None] * jnp.einsum('bhtk,bhvk->bhtv', q, C) + jnp.einsum('bhts,bhsv->bhtv', s, v)
    den = w_prev * jnp.einsum('bhtk,bhk->bht', q, n) + jnp.sum(s, -1)
    h = num / jnp.maximum(jnp.abs(den), jnp.exp(-m_t))[..., None]
    m_new = m_t[..., -1]
    w_c = jnp.exp(b[..., -1] + m - m_new)
    w_s = jnp.exp(b[..., -1:] - b + ig - m_new[..., None])
    C_new = w_c[..., None, None] * C + jnp.einsum('bhsv,bhsk->bhvk', v * w_s[..., None], k)
    n_new = w_c[..., None] * n + jnp.einsum('bhs,bhsk->bhk', w_s, k)
    return (C_new, n_new, m_new), h


def _mlstm_mixer(u, C, n, m, conv_buf, b_gates, conv_w, norm_g, chunk):
    B, T, _ = u.shape
    nc = T // chunk
    qk, conv_new = _causal_conv(u[..., :2 * M_WIDTH], conv_buf, conv_w)
    qk = jax.nn.silu(qk)
    v = u[..., 2 * M_WIDTH:3 * M_WIDTH]
    o = u[..., 3 * M_WIDTH:4 * M_WIDTH]
    gates = u[..., 4 * M_WIDTH:] + b_gates

    def to_chunks(z):
        return z.reshape(B, nc, chunk, M_HEADS, M_HEAD_DIM).transpose(1, 0, 3, 2, 4)

    def gate_chunks(g):
        return g.reshape(B, nc, chunk, M_HEADS).transpose(1, 0, 3, 2)

    q = to_chunks(qk[..., :M_WIDTH])
    k = to_chunks(qk[..., M_WIDTH:]) * (M_HEAD_DIM ** -0.5)
    ig = gate_chunks(gates[..., :M_HEADS])
    lf = gate_chunks(jax.nn.log_sigmoid(gates[..., M_HEADS:]))
    (C1, n1, m1), h = lax.scan(_mlstm_chunk_step, (C, n, m), (q, k, to_chunks(v), ig, lf))
    h = h.transpose(1, 0, 3, 2, 4).reshape(B, T, M_HEADS, M_HEAD_DIM)
    mu = jnp.mean(h, -1, keepdims=True)
    var = jnp.mean(jnp.square(h - mu), -1, keepdims=True)
    h = ((h - mu) * lax.rsqrt(var + LN_EPS)).reshape(B, T, M_WIDTH) * norm_g
    return h * jax.nn.sigmoid(o), (C1, n1, m1, conv_new)


def _t5_bucket(dist):
    exact = REL_BUCKETS // 2
    d = jnp.maximum(dist, 0)
    far = exact + (jnp.log(jnp.maximum(d, 1).astype(jnp.float32) / exact)
                   / math.log(REL_MAX_DIST / exact) * (REL_BUCKETS - exact)).astype(jnp.int32)
    return jnp.where(d < exact, d, jnp.minimum(far, REL_BUCKETS - 1))


def _dsa_split(u):
    B, T, _ = u.shape
    o = [0]
    for w in A_SPLIT:
        o.append(o[-1] + w)
    q, k, v, qi, ki, wi = [u[..., o[i]:o[i + 1]] for i in range(6)]
    return (q.reshape(B, T, A_HEADS, A_HEAD_DIM), k.reshape(B, T, A_HEADS, A_HEAD_DIM),
            v.reshape(B, T, A_HEADS, A_HEAD_DIM), qi.reshape(B, T, IDX_HEADS, IDX_DIM), ki, wi)


def _index_scores(qi, wi, ki):
    s = jax.nn.relu(jnp.einsum('bthd,bsd->bths', qi, ki) * (IDX_DIM ** -0.5))
    return jnp.einsum('bth,bths->bts', wi * (IDX_HEADS ** -0.5), s)


def _sparse_attend(q, kg, vg, sel, tpos, rel_bias):
    dist = tpos[None, :, None] - sel
    bias = jnp.swapaxes(rel_bias[_t5_bucket(dist)], -1, -2)
    logits = jnp.einsum('bthd,btkhd->bthk', q, kg) * (A_HEAD_DIM ** -0.5) + bias
    logits = jnp.where((dist >= 0)[:, :, None, :], logits, -jnp.inf)
    p = jax.nn.softmax(logits, axis=-1)
    return jnp.einsum('bthk,btkhd->bthd', p, vg)


def _dsa_prompt(q, k, v, qi, ki, wi, rel_bias):
    B, T = q.shape[:2]
    topk = min(IDX_TOPK, T // 4)
    spos = jnp.arange(T)

    def block(i):
        t0 = i * Q_BLOCK
        tpos = t0 + jnp.arange(Q_BLOCK)
        qb = lax.dynamic_slice_in_dim(q, t0, Q_BLOCK, 1)
        qib = lax.dynamic_slice_in_dim(qi, t0, Q_BLOCK, 1)
        wib = lax.dynamic_slice_in_dim(wi, t0, Q_BLOCK, 1)
        sc = _index_scores(qib, wib, ki)
        sc = jnp.where(spos[None, None, :] <= tpos[None, :, None], sc, -jnp.inf)
        _, sel = lax.top_k(sc, topk)
        kg = jax.vmap(lambda a, s: a[s])(k, sel)
        vg = jax.vmap(lambda a, s: a[s])(v, sel)
        return _sparse_attend(qb, kg, vg, sel, tpos, rel_bias)

    out = lax.map(block, jnp.arange(T // Q_BLOCK))
    return out.transpose(1, 0, 2, 3, 4).reshape(B, T, A_WIDTH)


def _dsa_sample(q, k, v, qi, ki, wi, k_pool, v_pool, ki_pool, page_table, rel_bias):
    DB, T = q.shape[:2]
    past = page_table.shape[1] * PAGE_SIZE
    L = past + T
    topk = min(IDX_TOPK, L // 4)
    tpos = past + jnp.arange(T)
    ki_past = ki_pool[page_table].reshape(DB, past, IDX_DIM)
    ki_all = jnp.concatenate([ki_past, ki], axis=1)
    sc = _index_scores(qi, wi, ki_all)
    sc = jnp.where(jnp.arange(L)[None, None, :] <= tpos[None, :, None], sc, -jnp.inf)
    _, sel = lax.top_k(sc, topk)
    in_past = sel < past
    sp = jnp.minimum(sel, past - 1)
    phys = jax.vmap(lambda pt, s: pt[s // PAGE_SIZE])(page_table, sp)
    row = sp % PAGE_SIZE
    sn = jnp.clip(sel - past, 0, T - 1)

    def pick(pool, new):
        from_pool = pool[phys, row]
        from_new = jax.vmap(lambda a, s: a[s])(new, sn)
        return jnp.where(in_past[..., None, None], from_pool, from_new)

    out = _sparse_attend(q, pick(k_pool, k), pick(v_pool, v), sel, tpos, rel_bias)
    return out.reshape(DB, T, A_WIDTH)


def _memory_attend(qm, mk, mv):
    B, T, _ = qm.shape
    qh = qm.reshape(B, T, X_HEADS, X_HEAD_DIM)
    logits = jnp.einsum('bthd,bshd->bhts', qh, mk) * (X_HEAD_DIM ** -0.5)
    p = jax.nn.softmax(logits, axis=-1)
    return jnp.einsum('bhts,bshd->bthd', p, mv).reshape(B, T, X_WIDTH)


def _moe_route(x, wg, bg, we, be):
    hp = lax.Precision.HIGHEST
    g_logits = jnp.dot(x, wg, precision=hp) + bg
    g_prob = jax.nn.softmax(g_logits, axis=-1)
    g_sel = jnp.argmax(g_prob, axis=-1)
    g_w = jnp.take_along_axis(g_prob, g_sel[:, None], axis=1)
    e_logits = (jnp.dot(x, we, precision=hp) + be).reshape(-1, N_GROUPS, EXP_PER_GROUP)
    e_logits = jnp.take_along_axis(e_logits, g_sel[:, None, None], axis=1)[:, 0]
    top_p, top_i = lax.top_k(jax.nn.softmax(e_logits, axis=-1), 2)
    top_p = top_p / jnp.sum(top_p, -1, keepdims=True)
    ids = (g_sel[:, None] * EXP_PER_GROUP + top_i).astype(jnp.int32)
    return ids, g_w * top_p


def _moe_dispatch(ids, wts, tile):
    n = ids.shape[0]
    n_asg = 2 * n
    n_slots = -(-(n_asg + N_EXPERTS * (tile - 1)) // tile) * tile
    n_tiles = n_slots // tile
    flat_e = ids.reshape(-1)
    onehot = (flat_e[:, None] == jnp.arange(N_EXPERTS, dtype=jnp.int32)[None, :]).astype(jnp.int32)
    rank = jnp.take_along_axis(jnp.cumsum(onehot, axis=0) - onehot, flat_e[:, None], axis=1)[:, 0]
    counts = jnp.sum(onehot, axis=0)
    padded = -(-counts // tile) * tile
    starts = jnp.cumsum(padded) - padded
    pos = (starts[flat_e] + rank).astype(jnp.int32)
    tok = jnp.arange(n_asg, dtype=jnp.int32) // 2
    row_src = jnp.zeros((n_slots,), jnp.int32).at[pos].set(tok)
    row_w = jnp.zeros((n_slots,), jnp.float32).at[pos].set(wts.reshape(-1))
    ends = jnp.cumsum(padded)
    tile_start = jnp.arange(n_tiles, dtype=jnp.int32) * tile
    tile_expert = jnp.sum((tile_start[:, None] >= ends[None, :]).astype(jnp.int32), axis=1)
    tile_valid = (tile_expert < N_EXPERTS).astype(jnp.int32)
    last_used = jnp.max(jnp.where(counts > 0, jnp.arange(N_EXPERTS), 0)).astype(jnp.int32)
    tile_expert = jnp.where(tile_valid == 1, tile_expert, last_used).astype(jnp.int32)
    return row_src, row_w[:, None], tile_expert, tile_valid, pos.reshape(n, 2)


def _moe_expert_kernel(te_ref, tv_ref, x_ref, rw_ref, wg_ref, wu_ref, wd_ref, y_ref):
    i = pl.program_id(0)

    @pl.when(tv_ref[i] == 1)
    def _():
        x = x_ref[...]
        hg = jnp.dot(x, wg_ref[0].astype(jnp.bfloat16), preferred_element_type=jnp.float32)
        hu = jnp.dot(x, wu_ref[0].astype(jnp.bfloat16), preferred_element_type=jnp.float32)
        h = (hg * jax.nn.sigmoid(hg)) * hu * rw_ref[...]
        y_ref[...] = jnp.dot(h.astype(jnp.bfloat16), wd_ref[0].astype(jnp.bfloat16),
                             preferred_element_type=jnp.float32)

    @pl.when(tv_ref[i] == 0)
    def _():
        y_ref[...] = jnp.zeros_like(y_ref)


def _moe_experts(x_sorted, row_w, tile_expert, tile_valid, w_gate, w_up, w_down, tile):
    n_slots, d = x_sorted.shape
    f = w_gate.shape[-1]
    grid_spec = pltpu.PrefetchScalarGridSpec(
        num_scalar_prefetch=2,
        grid=(n_slots // tile,),
        in_specs=[
            pl.BlockSpec((tile, d), lambda i, te, tv: (i, 0)),
            pl.BlockSpec((tile, 1), lambda i, te, tv: (i, 0)),
            pl.BlockSpec((1, d, f), lambda i, te, tv: (te[i], 0, 0)),
            pl.BlockSpec((1, d, f), lambda i, te, tv: (te[i], 0, 0)),
            pl.BlockSpec((1, f, d), lambda i, te, tv: (te[i], 0, 0)),
        ],
        out_specs=pl.BlockSpec((tile, d), lambda i, te, tv: (i, 0)),
    )
    return pl.pallas_call(
        _moe_expert_kernel,
        grid_spec=grid_spec,
        out_shape=jax.ShapeDtypeStruct((n_slots, d), jnp.float32),
        compiler_params=pltpu.CompilerParams(dimension_semantics=("arbitrary",)),
        name="moe_experts",
    )(tile_expert, tile_valid, x_sorted, row_w, w_gate, w_up, w_down)


def _hier_moe(x, wg, bg, we, be, w_up, w_gate, w_down):
    ids, wts = _moe_route(x, wg, bg, we, be)
    row_src, row_w, tile_expert, tile_valid, pos = _moe_dispatch(ids, wts, MOE_TILE_ROWS)
    x_sorted = x.astype(jnp.bfloat16)[row_src]
    y_sorted = _moe_experts(x_sorted, row_w, tile_expert, tile_valid, w_gate, w_up, w_down, MOE_TILE_ROWS)
    return y_sorted[pos[:, 0]] + y_sorted[pos[:, 1]]


def _residual_layer(x, tok, mem_out, w_o, g1, b1, g2, b2, wg, bg, we, be, wu, wgt, wd):
    y = jnp.concatenate([tok, mem_out], axis=-1) @ w_o
    x = _layer_norm(DN_ALPHA * x + y, g1, b1)
    shp = x.shape
    ffn = _hier_moe(x.reshape(-1, D_MODEL), wg, bg, we, be, wu, wgt, wd).reshape(shp)
    return _layer_norm(DN_ALPHA * x + ffn, g2, b2)


def kernel(x_prompt, x_sample, cache_k, cache_v, cache_kidx, state_C, state_n, state_m, state_conv,
           cache_mem_k, cache_mem_v, page_table, mem_prompt, w_in_mlstm, b_gates_mlstm, conv_mlstm,
           norm_mlstm, w_in_dsa, rel_bias, w_mem_kv, w_out, ln_mix_g, ln_mix_b, ln_ffn_g, ln_ffn_b,
           w_router_group, b_router_group, w_router_expert, b_router_expert, w_exp_up, w_exp_gate,
           w_exp_down):
    xp, xs = x_prompt, x_sample
    bp, tp = xp.shape[0], xp.shape[1]
    ts = xs.shape[1]
    kp, vp, kip, ksm, vsm, kism = [], [], [], [], [], []
    Cp, nP, mP, cP, Cs, nS, mS, cS = [], [], [], [], [], [], [], []
    mkp, mvp = [], []
    for l in range(DEPTH):
        j = l // N_MIXERS
        kv = mem_prompt @ w_mem_kv[l]
        mk_p = kv[..., :X_WIDTH].reshape(bp, -1, X_HEADS, X_HEAD_DIM)
        mv_p = kv[..., X_WIDTH:].reshape(bp, -1, X_HEADS, X_HEAD_DIM)
        mkp.append(mk_p)
        mvp.append(mv_p)
        if l % N_MIXERS == 0:
            up_p = xp @ w_in_mlstm[j]
            up_s = xs @ w_in_mlstm[j]
            tok_p, st_p = _mlstm_mixer(
                up_p[..., :M_TOK_PROJ],
                jnp.zeros((bp, M_HEADS, M_HEAD_DIM, M_HEAD_DIM), jnp.float32),
                jnp.zeros((bp, M_HEADS, M_HEAD_DIM), jnp.float32),
                jnp.zeros((bp, M_HEADS), jnp.float32),
                jnp.zeros((bp, CONV_WIDTH - 1, 2 * M_WIDTH), xp.dtype),
                b_gates_mlstm[j], conv_mlstm[j], norm_mlstm[j], M_CHUNK)
            tok_s, st_s = _mlstm_mixer(up_s[..., :M_TOK_PROJ], state_C[j], state_n[j], state_m[j], state_conv[j],
                                       b_gates_mlstm[j], conv_mlstm[j], norm_mlstm[j], ts)
            Cp.append(st_p[0]); nP.append(st_p[1]); mP.append(st_p[2]); cP.append(st_p[3])
            Cs.append(st_s[0]); nS.append(st_s[1]); mS.append(st_s[2]); cS.append(st_s[3])
        else:
            up_p = xp @ w_in_dsa[j]
            up_s = xs @ w_in_dsa[j]
            q, k, v, qi, ki, wi = _dsa_split(up_p[..., :A_TOK_PROJ])
            tok_p = _dsa_prompt(q, k, v, qi, ki, wi, rel_bias)
            kp.append(k); vp.append(v); kip.append(ki)
            q, k, v, qi, ki, wi = _dsa_split(up_s[..., :A_TOK_PROJ])
            tok_s = _dsa_sample(q, k, v, qi, ki, wi, cache_k[j], cache_v[j], cache_kidx[j], page_table, rel_bias)
            ksm.append(k); vsm.append(v); kism.append(ki)
        mo_p = _memory_attend(up_p[..., -X_WIDTH:], mk_p, mv_p)
        mo_s = _memory_attend(up_s[..., -X_WIDTH:], cache_mem_k[l], cache_mem_v[l])
        args = (w_out[l], ln_mix_g[l], ln_mix_b[l], ln_ffn_g[l], ln_ffn_b[l],
                w_router_group[l], b_router_group[l], w_router_expert[l], b_router_expert[l],
                w_exp_up[l], w_exp_gate[l], w_exp_down[l])
        xp = _residual_layer(xp, tok_p, mo_p, *args)
        xs = _residual_layer(xs, tok_s, mo_s, *args)
    return (xp, xs,
            jnp.stack(kp), jnp.stack(vp), jnp.stack(kip),
            jnp.stack(ksm), jnp.stack(vsm), jnp.stack(kism),
            jnp.stack(Cp), jnp.stack(nP), jnp.stack(mP), jnp.stack(cP),
            jnp.stack(Cs), jnp.stack(nS), jnp.stack(mS), jnp.stack(cS),
            jnp.stack(mkp), jnp.stack(mvp))
```

```python
import functools
import math

import jax
import jax.numpy as jnp
from jax import lax
from jax.experimental import pallas as pl
from jax.experimental.pallas import tpu as pltpu

D_MODEL = 1024
DEPTH = 4
PAGE_SIZE = 128
N_MIXERS = 2
M_HEADS = 4
M_HEAD_DIM = 128
M_WIDTH = M_HEADS * M_HEAD_DIM
CONV_WIDTH = 4
M_CHUNK = 64
M_TOK_PROJ = 4 * M_WIDTH + 2 * M_HEADS
A_HEADS = 8
A_HEAD_DIM = 64
A_WIDTH = A_HEADS * A_HEAD_DIM
IDX_HEADS = 8
IDX_DIM = 64
IDX_TOPK = 256
Q_BLOCK = 128
A_SPLIT = (A_WIDTH, A_WIDTH, A_WIDTH, IDX_HEADS * IDX_DIM, IDX_DIM, IDX_HEADS)
A_TOK_PROJ = sum(A_SPLIT)
REL_BUCKETS = 32
REL_MAX_DIST = 128
X_HEADS = 4
X_HEAD_DIM = 128
X_WIDTH = X_HEADS * X_HEAD_DIM
N_GROUPS = 4
EXP_PER_GROUP = 8
N_EXPERTS = N_GROUPS * EXP_PER_GROUP
D_EXPERT = 256
DN_ALPHA = (2 * DEPTH) ** 0.25
LN_EPS = 1e-5

MOE_TILE_ROWS = 256


def _layer_norm(x, g, b):
    mu = jnp.mean(x, -1, keepdims=True)
    var = jnp.mean(jnp.square(x - mu), -1, keepdims=True)
    return (x - mu) * lax.rsqrt(var + LN_EPS) * g + b


def _causal_conv(u, buf, w):
    T = u.shape[1]
    ext = jnp.concatenate([buf, u], axis=1)
    out = sum(ext[:, j:j + T] * w[j] for j in range(CONV_WIDTH))
    return out, ext[:, T:]


def _mlstm_chunk_step(carry, inp):
    C, n, m = carry
    q, k, v, ig, lf = inp
    L = q.shape[2]
    b = jnp.cumsum(lf, axis=-1)
    causal = jnp.tril(jnp.ones((L, L), dtype=bool))
    logd = jnp.where(causal, b[..., :, None] - b[..., None, :] + ig[..., None, :], -jnp.inf)
    log_prev = b + m[..., None]
    m_t = jnp.maximum(log_prev, jnp.max(logd, axis=-1))
    d = jnp.exp(logd - m_t[..., None])
    w_prev = jnp.exp(log_prev - m_t)
    s = jnp.einsum('bhtd,bhsd->bhts', q, k) * d
    num = w_prev[..., None] * jnp.einsum('bhtk,bhvk->bhtv', q, C) + jnp.einsum('bhts,bhsv->bhtv', s, v)
    den = w_prev * jnp.einsum('bhtk,bhk->bht', q, n) + jnp.sum(s, -1)
    h = num / jnp.maximum(jnp.abs(den), jnp.exp(-m_t))[..., None]
    m_new = m_t[..., -1]
    w_c = jnp.exp(b[..., -1] + m - m_new)
    w_s = jnp.exp(b[..., -1:] - b + ig - m_new[..., None])
    C_new = w_c[..., None, None] * C + jnp.einsum('bhsv,bhsk->bhvk', v * w_s[..., None], k)
    n_new = w_c[..., None] * n + jnp.einsum('bhs,bhsk->bhk', w_s, k)
    return (C_new, n_new, m_new), h


def _mlstm_mixer(u, C, n, m, conv_buf, b_gates, conv_w, norm_g, chunk):
    B, T, _ = u.shape
    nc = T // chunk
    qk, conv_new = _causal_conv(u[..., :2 * M_WIDTH], conv_buf, conv_w)
    qk = jax.nn.silu(qk)
    v = u[..., 2 * M_WIDTH:3 * M_WIDTH]
    o = u[..., 3 * M_WIDTH:4 * M_WIDTH]
    gates = u[..., 4 * M_WIDTH:] + b_gates

    def to_chunks(z):
        return z.reshape(B, nc, chunk, M_HEADS, M_HEAD_DIM).transpose(1, 0, 3, 2, 4)

    def gate_chunks(g):
        return g.reshape(B, nc, chunk, M_HEADS).transpose(1, 0, 3, 2)

    q = to_chunks(qk[..., :M_WIDTH])
    k = to_chunks(qk[..., M_WIDTH:]) * (M_HEAD_DIM ** -0.5)
    ig = gate_chunks(gates[..., :M_HEADS])
    lf = gate_chunks(jax.nn.log_sigmoid(gates[..., M_HEADS:]))
    (C1, n1, m1), h = lax.scan(_mlstm_chunk_step, (C, n, m), (q, k, to_chunks(v), ig, lf))
    h = h.transpose(1, 0, 3, 2, 4).reshape(B, T, M_HEADS, M_HEAD_DIM)
    mu = jnp.mean(h, -1, keepdims=True)
    var = jnp.mean(jnp.square(h - mu), -1, keepdims=True)
    h = ((h - mu) * lax.rsqrt(var + LN_EPS)).reshape(B, T, M_WIDTH) * norm_g
    return h * jax.nn.sigmoid(o), (C1, n1, m1, conv_new)


def _t5_bucket(dist):
    exact = REL_BUCKETS // 2
    d = jnp.maximum(dist, 0)
    far = exact + (jnp.log(jnp.maximum(d, 1).astype(jnp.float32) / exact)
                   / math.log(REL_MAX_DIST / exact) * (REL_BUCKETS - exact)).astype(jnp.int32)
    return jnp.where(d < exact, d, jnp.minimum(far, REL_BUCKETS - 1))


def _dsa_split(u):
    B, T, _ = u.shape
    o = [0]
    for w in A_SPLIT:
        o.append(o[-1] + w)
    q, k, v, qi, ki, wi = [u[..., o[i]:o[i + 1]] for i in range(6)]
    return (q.reshape(B, T, A_HEADS, A_HEAD_DIM), k.reshape(B, T, A_HEADS, A_HEAD_DIM),
            v.reshape(B, T, A_HEADS, A_HEAD_DIM), qi.reshape(B, T, IDX_HEADS, IDX_DIM), ki, wi)


def _index_scores(qi, wi, ki):
    s = jax.nn.relu(jnp.einsum('bthd,bsd->bths', qi, ki) * (IDX_DIM ** -0.5))
    return jnp.einsum('bth,bths->bts', wi * (IDX_HEADS ** -0.5), s)


def _sparse_attend(q, kg, vg, sel, tpos, rel_bias):
    dist = tpos[None, :, None] - sel
    bias = jnp.swapaxes(rel_bias[_t5_bucket(dist)], -1, -2)
    logits = jnp.einsum('bthd,btkhd->bthk', q, kg) * (A_HEAD_DIM ** -0.5) + bias
    logits = jnp.where((dist >= 0)[:, :, None, :], logits, -jnp.inf)
    p = jax.nn.softmax(logits, axis=-1)
    return jnp.einsum('bthk,btkhd->bthd', p, vg)


DSA_CHUNK = 2 * Q_BLOCK
MASK_NEG = -1e30
_INT_MIN = -2 ** 31
assert REL_BUCKETS // 2 + int(math.log((Q_BLOCK + 1) / (REL_BUCKETS // 2)) / math.log(REL_MAX_DIST / (REL_BUCKETS // 2))
                              * (REL_BUCKETS - REL_BUCKETS // 2)) >= REL_BUCKETS - 1


def _sortable_key(x):
    b = lax.bitcast_convert_type(x, jnp.int32)
    return b ^ ((b >> 31) & 0x7FFFFFFF)


def _dsa_prompt_kernel(q_ref, qi_ref, wi_ref, k_ref, v_ref, kk_ref, bias_ref, o_ref, key_ref, last_tie_ref,
                       *, topk, idx_bits):
    i = pl.program_id(1)
    t0 = i * Q_BLOCK
    nch = (i + 2) // 2
    f32, bf16, i32 = jnp.float32, jnp.bfloat16, jnp.int32
    row = lax.broadcasted_iota(i32, (Q_BLOCK, 1), 0)
    tpos = t0 + row
    lane_c = lax.broadcasted_iota(i32, (Q_BLOCK, DSA_CHUNK), 1)
    lane_h = lax.broadcasted_iota(i32, (Q_BLOCK, 128), 1)
    lo_half = lane_h < A_HEAD_DIM
    nt = (((1,), (1,)), ((), ()))

    def head_pair_split(x):
        return jnp.where(lo_half, x, jnp.zeros_like(x)), jnp.where(lo_half, jnp.zeros_like(x), x)

    qi = qi_ref[0]
    wi = wi_ref[0] * (IDX_HEADS ** -0.5 * IDX_DIM ** -0.5)
    qi_heads = []
    for p in range(IDX_HEADS // 2):
        qi_heads.extend(head_pair_split(qi[:, 128 * p:128 * (p + 1)]))

    def score_chunk(c, carry):
        kk = kk_ref[0, pl.ds(pl.multiple_of(c * DSA_CHUNK, DSA_CHUNK), DSA_CHUNK), :]
        acc = jnp.zeros((Q_BLOCK, DSA_CHUNK), f32)
        for h in range(IDX_HEADS):
            s = lax.dot_general(qi_heads[h], kk, nt, preferred_element_type=f32)
            acc = acc + jnp.maximum(s, 0.0) * wi[:, h:h + 1]
        spos = c * DSA_CHUNK + lane_c
        key_ref[c] = jnp.where(spos <= tpos, _sortable_key(acc), _INT_MIN)
        return carry

    lax.fori_loop(0, nch, score_chunk, 0)

    k_row = jnp.minimum(topk, tpos + 1).astype(f32)

    def count(pred):
        def body(c, cnt):
            x = jnp.where(pred(key_ref[c], c), 1.0, 0.0)
            return cnt + x[:, :128] + x[:, 128:]
        cnt = lax.fori_loop(0, nch, body, jnp.zeros((Q_BLOCK, 128), f32))
        return jnp.sum(cnt, axis=1, keepdims=True)

    def thr_bit(it, tau_u):
        bit = jnp.left_shift(jnp.int32(1), 31 - it)
        try_u = tau_u | bit
        try_s = try_u ^ _INT_MIN
        ok = count(lambda key, c: key >= try_s) >= k_row
        return jnp.where(ok, try_u, tau_u)

    tau = lax.fori_loop(0, 32, thr_bit, jnp.zeros((Q_BLOCK, 1), i32)) ^ _INT_MIN
    need = k_row - count(lambda key, c: key > tau)

    def pos_bit(it, x):
        bit = jnp.left_shift(jnp.int32(1), idx_bits - 1 - it)
        try_x = x | bit
        ok = count(lambda key, c: (key == tau) & (c * DSA_CHUNK + lane_c < try_x)) < need
        return jnp.where(ok, try_x, x)

    n_tie = count(lambda key, c: key == tau)
    last_tie_ref[...] = jnp.full((Q_BLOCK, 1), 2 ** idx_bits, i32)

    @pl.when(jnp.max(n_tie - need) > 0.0)
    def _():
        last_tie_ref[...] = lax.fori_loop(0, idx_bits, pos_bit, jnp.zeros((Q_BLOCK, 1), i32))

    last_tie = last_tie_ref[...]

    q = q_ref[0]
    q_heads = []
    for p in range(A_HEADS // 2):
        q_heads.extend(head_pair_split(q[:, 128 * p:128 * (p + 1)]))
    scale = A_HEAD_DIM ** -0.5

    def attend_chunk(c, carry):
        ms, ls, accs = carry
        key = key_ref[c]
        spos = c * DSA_CHUNK + lane_c
        sel = (key > tau) | ((key == tau) & (spos <= last_tie))
        maskbias = jnp.where(sel, 0.0, MASK_NEG)
        off = pl.multiple_of(c * DSA_CHUNK, DSA_CHUNK)
        b_left = jnp.clip(2 - (i - 2 * c), 0, 2)
        b_right = jnp.clip(3 - (i - 2 * c), 0, 2)
        ms2, ls2, accs2 = [], [], []
        for p in range(A_HEADS // 2):
            kc = k_ref[0, pl.ds(off, DSA_CHUNK), 128 * p:128 * (p + 1)]
            vc = v_ref[0, pl.ds(off, DSA_CHUNK), 128 * p:128 * (p + 1)]
            pv = []
            for half in range(2):
                h = 2 * p + half
                s = lax.dot_general(q_heads[h], kc, nt, preferred_element_type=f32) * scale
                bias = jnp.concatenate([bias_ref[b_left, h], bias_ref[b_right, h]], axis=1)
                s = s + bias + maskbias
                m_new = jnp.maximum(ms[h], jnp.max(s, axis=1, keepdims=True))
                alpha = jnp.exp(ms[h] - m_new)
                pr = jnp.exp(s - m_new)
                ls2.append(alpha * ls[h] + jnp.sum(pr, axis=1, keepdims=True))
                ms2.append(m_new)
                pv.append((alpha, jnp.dot(pr.astype(bf16), vc, preferred_element_type=f32)))
            accs2.append(jnp.where(lo_half, pv[0][0] * accs[p] + pv[0][1], pv[1][0] * accs[p] + pv[1][1]))
        return ms2, ls2, accs2

    init = ([jnp.full((Q_BLOCK, 1), MASK_NEG, f32)] * A_HEADS,
            [jnp.zeros((Q_BLOCK, 1), f32)] * A_HEADS,
            [jnp.zeros((Q_BLOCK, 128), f32)] * (A_HEADS // 2))
    ms, ls, accs = lax.fori_loop(0, nch, attend_chunk, init)
    for p in range(A_HEADS // 2):
        o_ref[0, :, 128 * p:128 * (p + 1)] = accs[p] / jnp.where(lo_half, ls[2 * p], ls[2 * p + 1])


def _dsa_bias_tiles(rel_bias):
    r = jnp.arange(Q_BLOCK)[:, None]
    c = jnp.arange(Q_BLOCK)[None, :]
    tiles = []
    for off in (2, 1, 0):
        dist = off * Q_BLOCK + r - c
        tiles.append(jnp.moveaxis(rel_bias[_t5_bucket(dist)], -1, 0))
    return jnp.stack(tiles).astype(jnp.float32)


def _dsa_prompt(q, k, v, qi, ki, wi, rel_bias):
    B, T, _ = q.shape
    assert T % DSA_CHUNK == 0
    topk = min(IDX_TOPK, T // 4)
    bf16 = jnp.bfloat16
    kk = jnp.concatenate([ki, ki], axis=-1).astype(bf16)
    kern = functools.partial(_dsa_prompt_kernel, topk=topk, idx_bits=max(1, (T - 1).bit_length()))
    qspec = pl.BlockSpec((1, Q_BLOCK, A_WIDTH), lambda b, i: (b, i, 0))
    full = lambda w: pl.BlockSpec((1, T, w), lambda b, i: (b, 0, 0))
    return pl.pallas_call(
        kern,
        grid=(B, T // Q_BLOCK),
        in_specs=[qspec, qspec, pl.BlockSpec((1, Q_BLOCK, IDX_HEADS), lambda b, i: (b, i, 0)),
                  full(A_WIDTH), full(A_WIDTH), full(2 * IDX_DIM),
                  pl.BlockSpec((3, A_HEADS, Q_BLOCK, Q_BLOCK), lambda b, i: (0, 0, 0, 0))],
        out_specs=qspec,
        out_shape=jax.ShapeDtypeStruct((B, T, A_WIDTH), jnp.float32),
        scratch_shapes=[pltpu.VMEM((T // DSA_CHUNK, Q_BLOCK, DSA_CHUNK), jnp.int32),
                        pltpu.VMEM((Q_BLOCK, 1), jnp.int32)],
        compiler_params=pltpu.CompilerParams(dimension_semantics=("arbitrary", "arbitrary")),
        name="dsa_prompt",
    )(q.astype(bf16), qi.astype(bf16), wi, k.astype(bf16), v.astype(bf16), kk, _dsa_bias_tiles(rel_bias))


def _dsa_sample(q, k, v, qi, ki, wi, k_pool, v_pool, ki_pool, page_table, rel_bias):
    DB, T = q.shape[:2]
    past = page_table.shape[1] * PAGE_SIZE
    L = past + T
    topk = min(IDX_TOPK, L // 4)
    tpos = past + jnp.arange(T)
    ki_past = ki_pool[page_table].reshape(DB, past, IDX_DIM)
    ki_all = jnp.concatenate([ki_past, ki], axis=1)
    sc = _index_scores(qi, wi, ki_all)
    sc = jnp.where(jnp.arange(L)[None, None, :] <= tpos[None, :, None], sc, -jnp.inf)
    _, sel = lax.top_k(sc, topk)
    in_past = sel < past
    sp = jnp.minimum(sel, past - 1)
    phys = jax.vmap(lambda pt, s: pt[s // PAGE_SIZE])(page_table, sp)
    row = sp % PAGE_SIZE
    sn = jnp.clip(sel - past, 0, T - 1)

    def pick(pool, new):
        from_pool = pool[phys, row]
        from_new = jax.vmap(lambda a, s: a[s])(new, sn)
        return jnp.where(in_past[..., None, None], from_pool, from_new)

    out = _sparse_attend(q, pick(k_pool, k), pick(v_pool, v), sel, tpos, rel_bias)
    return out.reshape(DB, T, A_WIDTH)


def _memory_attend(qm, mk, mv):
    B, T, _ = qm.shape
    qh = qm.reshape(B, T, X_HEADS, X_HEAD_DIM)
    logits = jnp.einsum('bthd,bshd->bhts', qh, mk) * (X_HEAD_DIM ** -0.5)
    p = jax.nn.softmax(logits, axis=-1)
    return jnp.einsum('bhts,bshd->bthd', p, mv).reshape(B, T, X_WIDTH)


def _moe_route(x, wg, bg, we, be):
    hp = lax.Precision.HIGHEST
    g_logits = jnp.dot(x, wg, precision=hp) + bg
    g_prob = jax.nn.softmax(g_logits, axis=-1)
    g_sel = jnp.argmax(g_prob, axis=-1)
    g_w = jnp.take_along_axis(g_prob, g_sel[:, None], axis=1)
    e_logits = (jnp.dot(x, we, precision=hp) + be).reshape(-1, N_GROUPS, EXP_PER_GROUP)
    e_logits = jnp.take_along_axis(e_logits, g_sel[:, None, None], axis=1)[:, 0]
    top_p, top_i = lax.top_k(jax.nn.softmax(e_logits, axis=-1), 2)
    top_p = top_p / jnp.sum(top_p, -1, keepdims=True)
    ids = (g_sel[:, None] * EXP_PER_GROUP + top_i).astype(jnp.int32)
    return ids, g_w * top_p


def _moe_dispatch(ids, wts, tile):
    n = ids.shape[0]
    n_asg = 2 * n
    n_slots = -(-(n_asg + N_EXPERTS * (tile - 1)) // tile) * tile
    n_tiles = n_slots // tile
    flat_e = ids.reshape(-1)
    onehot = (flat_e[:, None] == jnp.arange(N_EXPERTS, dtype=jnp.int32)[None, :]).astype(jnp.int32)
    rank = jnp.take_along_axis(jnp.cumsum(onehot, axis=0) - onehot, flat_e[:, None], axis=1)[:, 0]
    counts = jnp.sum(onehot, axis=0)
    padded = -(-counts // tile) * tile
    starts = jnp.cumsum(padded) - padded
    pos = (starts[flat_e] + rank).astype(jnp.int32)
    tok = jnp.arange(n_asg, dtype=jnp.int32) // 2
    row_src = jnp.zeros((n_slots,), jnp.int32).at[pos].set(tok)
    row_w = jnp.zeros((n_slots,), jnp.float32).at[pos].set(wts.reshape(-1))
    ends = jnp.cumsum(padded)
    tile_start = jnp.arange(n_tiles, dtype=jnp.int32) * tile
    tile_expert = jnp.sum((tile_start[:, None] >= ends[None, :]).astype(jnp.int32), axis=1)
    tile_valid = (tile_expert < N_EXPERTS).astype(jnp.int32)
    last_used = jnp.max(jnp.where(counts > 0, jnp.arange(N_EXPERTS), 0)).astype(jnp.int32)
    tile_expert = jnp.where(tile_valid == 1, tile_expert, last_used).astype(jnp.int32)
    return row_src, row_w[:, None], tile_expert, tile_valid, pos.reshape(n, 2)


def _moe_expert_kernel(te_ref, tv_ref, x_ref, rw_ref, wg_ref, wu_ref, wd_ref, y_ref):
    i = pl.program_id(0)

    @pl.when(tv_ref[i] == 1)
    def _():
        x = x_ref[...]
        hg = jnp.dot(x, wg_ref[0].astype(jnp.bfloat16), preferred_element_type=jnp.float32)
        hu = jnp.dot(x, wu_ref[0].astype(jnp.bfloat16), preferred_element_type=jnp.float32)
        h = (hg * jax.nn.sigmoid(hg)) * hu * rw_ref[...]
        y_ref[...] = jnp.dot(h.astype(jnp.bfloat16), wd_ref[0].astype(jnp.bfloat16),
                             preferred_element_type=jnp.float32)

    @pl.when(tv_ref[i] == 0)
    def _():
        y_ref[...] = jnp.zeros_like(y_ref)


def _moe_experts(x_sorted, row_w, tile_expert, tile_valid, w_gate, w_up, w_down, tile):
    n_slots, d = x_sorted.shape
    f = w_gate.shape[-1]
    grid_spec = pltpu.PrefetchScalarGridSpec(
        num_scalar_prefetch=2,
        grid=(n_slots // tile,),
        in_specs=[
            pl.BlockSpec((tile, d), lambda i, te, tv: (i, 0)),
            pl.BlockSpec((tile, 1), lambda i, te, tv: (i, 0)),
            pl.BlockSpec((1, d, f), lambda i, te, tv: (te[i], 0, 0)),
            pl.BlockSpec((1, d, f), lambda i, te, tv: (te[i], 0, 0)),
            pl.BlockSpec((1, f, d), lambda i, te, tv: (te[i], 0, 0)),
        ],
        out_specs=pl.BlockSpec((tile, d), lambda i, te, tv: (i, 0)),
    )
    return pl.pallas_call(
        _moe_expert_kernel,
        grid_spec=grid_spec,
        out_shape=jax.ShapeDtypeStruct((n_slots, d), jnp.float32),
        compiler_params=pltpu.CompilerParams(dimension_semantics=("arbitrary",)),
        name="moe_experts",
    )(tile_expert, tile_valid, x_sorted, row_w, w_gate, w_up, w_down)


def _hier_moe(x, wg, bg, we, be, w_up, w_gate, w_down):
    ids, wts = _moe_route(x, wg, bg, we, be)
    row_src, row_w, tile_expert, tile_valid, pos = _moe_dispatch(ids, wts, MOE_TILE_ROWS)
    x_sorted = x.astype(jnp.bfloat16)[row_src]
    y_sorted = _moe_experts(x_sorted, row_w, tile_expert, tile_valid, w_gate, w_up, w_down, MOE_TILE_ROWS)
    return y_sorted[pos[:, 0]] + y_sorted[pos[:, 1]]


def _residual_layer(x, tok, mem_out, w_o, g1, b1, g2, b2, wg, bg, we, be, wu, wgt, wd):
    y = jnp.concatenate([tok, mem_out], axis=-1) @ w_o
    x = _layer_norm(DN_ALPHA * x + y, g1, b1)
    shp = x.shape
    ffn = _hier_moe(x.reshape(-1, D_MODEL), wg, bg, we, be, wu, wgt, wd).reshape(shp)
    return _layer_norm(DN_ALPHA * x + ffn, g2, b2)


def kernel(x_prompt, x_sample, cache_k, cache_v, cache_kidx, state_C, state_n, state_m, state_conv,
           cache_mem_k, cache_mem_v, page_table, mem_prompt, w_in_mlstm, b_gates_mlstm, conv_mlstm,
           norm_mlstm, w_in_dsa, rel_bias, w_mem_kv, w_out, ln_mix_g, ln_mix_b, ln_ffn_g, ln_ffn_b,
           w_router_group, b_router_group, w_router_expert, b_router_expert, w_exp_up, w_exp_gate,
           w_exp_down):
    xp, xs = x_prompt, x_sample
    bp, tp = xp.shape[0], xp.shape[1]
    ts = xs.shape[1]
    kp, vp, kip, ksm, vsm, kism = [], [], [], [], [], []
    Cp, nP, mP, cP, Cs, nS, mS, cS = [], [], [], [], [], [], [], []
    mkp, mvp = [], []
    for l in range(DEPTH):
        j = l // N_MIXERS
        kv = mem_prompt @ w_mem_kv[l]
        mk_p = kv[..., :X_WIDTH].reshape(bp, -1, X_HEADS, X_HEAD_DIM)
        mv_p = kv[..., X_WIDTH:].reshape(bp, -1, X_HEADS, X_HEAD_DIM)
        mkp.append(mk_p)
        mvp.append(mv_p)
        if l % N_MIXERS == 0:
            up_p = xp @ w_in_mlstm[j]
            up_s = xs @ w_in_mlstm[j]
            tok_p, st_p = _mlstm_mixer(
                up_p[..., :M_TOK_PROJ],
                jnp.zeros((bp, M_HEADS, M_HEAD_DIM, M_HEAD_DIM), jnp.float32),
                jnp.zeros((bp, M_HEADS, M_HEAD_DIM), jnp.float32),
                jnp.zeros((bp, M_HEADS), jnp.float32),
                jnp.zeros((bp, CONV_WIDTH - 1, 2 * M_WIDTH), xp.dtype),
                b_gates_mlstm[j], conv_mlstm[j], norm_mlstm[j], M_CHUNK)
            tok_s, st_s = _mlstm_mixer(up_s[..., :M_TOK_PROJ], state_C[j], state_n[j], state_m[j], state_conv[j],
                                       b_gates_mlstm[j], conv_mlstm[j], norm_mlstm[j], ts)
            Cp.append(st_p[0]); nP.append(st_p[1]); mP.append(st_p[2]); cP.append(st_p[3])
            Cs.append(st_s[0]); nS.append(st_s[1]); mS.append(st_s[2]); cS.append(st_s[3])
        else:
            up_p = xp @ w_in_dsa[j]
            up_s = xs @ w_in_dsa[j]
            q, k, v, qi, ki, wi = _dsa_split(up_p[..., :A_TOK_PROJ])
            flat = lambda z: z.reshape(bp, tp, -1)
            tok_p = _dsa_prompt(flat(q), flat(k), flat(v), flat(qi), ki, wi, rel_bias)
            kp.append(k); vp.append(v); kip.append(ki)
            q, k, v, qi, ki, wi = _dsa_split(up_s[..., :A_TOK_PROJ])
            tok_s = _dsa_sample(q, k, v, qi, ki, wi, cache_k[j], cache_v[j], cache_kidx[j], page_table, rel_bias)
            ksm.append(k); vsm.append(v); kism.append(ki)
        mo_p = _memory_attend(up_p[..., -X_WIDTH:], mk_p, mv_p)
        mo_s = _memory_attend(up_s[..., -X_WIDTH:], cache_mem_k[l], cache_mem_v[l])
        args = (w_out[l], ln_mix_g[l], ln_mix_b[l], ln_ffn_g[l], ln_ffn_b[l],
                w_router_group[l], b_router_group[l], w_router_expert[l], b_router_expert[l],
                w_exp_up[l], w_exp_gate[l], w_exp_down[l])
        xp = _residual_layer(xp, tok_p, mo_p, *args)
        xs = _residual_layer(xs, tok_s, mo_s, *args)
    return (xp, xs,
            jnp.stack(kp), jnp.stack(vp), jnp.stack(kip),
            jnp.stack(ksm), jnp.stack(vsm), jnp.stack(kism),
            jnp.stack(Cp), jnp.stack(nP), jnp.stack(mP), jnp.stack(cP),
            jnp.stack(Cs), jnp.stack(nS), jnp.stack(mS), jnp.stack(cS),
            jnp.stack(mkp), jnp.stack(mvp))
```
